```python
import math
import jax, jax.numpy as jnp
from jax import lax
import numpy as np

D_MODEL = 1024
BATCH = 16
SEQ = 2048
DEPTH = 2
DEC_BATCH = 16
DEC_SEQ = 16
PAST_LEN = 4096

CHUNK = 64
N_A_LAYERS = DEPTH // 2
N_B_LAYERS = DEPTH - N_A_LAYERS
RWKV_HEAD = 64
RWKV_HEADS = D_MODEL // RWKV_HEAD
DECAY_LORA = 64
AAA_LORA = 64
GATE_LORA = 128
GN_EPS = 64e-5
DIFF_HEADS = 8
DIFF_DH = D_MODEL // (2 * DIFF_HEADS)
ROT_DIM = DIFF_DH // 4
ROPE_THETA = 500000.0
Q_BLOCK = 128
SUBLN_EPS = 1e-5
PEER_HEADS = 8
PEER_KEYS = 128
PEER_EXPERTS = PEER_KEYS * PEER_KEYS
PEER_DKEY = 256
PEER_DHALF = PEER_DKEY // 2
PEER_TOPK = 16
PEER_ROWS = 64
DN_ALPHA = (2.0 * DEPTH) ** 0.25
DN_BETA = (8.0 * DEPTH) ** -0.25
LN_EPS = 1e-5

kernel_name = "yoco_rwkv7_diffattn_peer_stream_step"

F32 = jnp.float32


def layer_norm(x, g, b):
    xf = x.astype(F32)
    mu = jnp.mean(xf, -1, keepdims=True)
    var = jnp.mean(jnp.square(xf - mu), -1, keepdims=True)
    return ((xf - mu) * lax.rsqrt(var + LN_EPS) * g.astype(F32) + b.astype(F32)).astype(x.dtype)


def rwkv7_time_mix(x, shift_prev, wkv_prev, mu, w_rkv, w0, w1, w2, a0, a1, a2, g1, g2,
                   k_k, k_a, r_k, lnx_g, lnx_b, w_out):
    B, T, D = x.shape
    H, N = RWKV_HEADS, RWKV_HEAD
    x_prev = jnp.concatenate([shift_prev.astype(x.dtype), x[:, :-1]], axis=1)
    xx = x_prev - x
    xr, xk, xv, xw, xa, xg = (x + xx * mu[i] for i in range(6))
    r = xr @ w_rkv[0]
    k = xk @ w_rkv[1]
    v = xv @ w_rkv[2]
    logw = -jax.nn.softplus(-(w0 + jnp.tanh(xw @ w1) @ w2)) - 0.5
    a = jax.nn.sigmoid(a0 + (xa @ a1) @ a2)
    g = jax.nn.sigmoid(xg @ g1) @ g2
    hd = lambda t: t.reshape(B, T, H, N).astype(F32)
    kk = hd(k * k_k)
    kk = kk / jnp.maximum(jnp.sqrt(jnp.sum(kk * kk, -1, keepdims=True)), 1e-12)
    k = k * (1.0 + (a - 1.0) * k_a)
    rf, kf, vf, af = hd(r), hd(k), hd(v), hd(a)
    decay = jnp.exp(-jnp.exp(hd(logw)))

    def step(S, inp):
        r_t, d_t, k_t, v_t, kk_t, a_t = inp
        Skk = jnp.einsum('bhvk,bhk->bhv', S, kk_t)
        S = (S * d_t[:, :, None, :]
             - Skk[..., None] * (kk_t * a_t)[:, :, None, :]
             + v_t[..., None] * k_t[:, :, None, :])
        y_t = jnp.einsum('bhvk,bhk->bhv', S, r_t)
        return S, y_t

    seq = tuple(jnp.moveaxis(t, 1, 0) for t in (rf, decay, kf, vf, kk, af))
    S_final, y = lax.scan(step, wkv_prev.astype(F32), seq)
    y = jnp.moveaxis(y, 0, 1)
    ym = jnp.mean(y, -1, keepdims=True)
    yv = jnp.mean(jnp.square(y - ym), -1, keepdims=True)
    y = ((y - ym) * lax.rsqrt(yv + GN_EPS)).reshape(B, T, D) * lnx_g.astype(F32) + lnx_b.astype(F32)
    bonus = jnp.sum(rf * kf * r_k.astype(F32), -1, keepdims=True) * vf
    y = y + bonus.reshape(B, T, D)
    out = (y * g.astype(F32)).astype(x.dtype) @ w_out
    return out, x[:, -1:], S_final.astype(x.dtype)


def rope_partial(t, pos):
    half = ROT_DIM // 2
    inv = jnp.power(ROPE_THETA, -jnp.arange(half, dtype=F32) * 2.0 / ROT_DIM)
    ang = pos.astype(F32)[:, None] * inv[None, :]
    cos = jnp.cos(ang)[None, :, None, None, :]
    sin = jnp.sin(ang)[None, :, None, None, :]
    tr = t[..., :ROT_DIM].astype(F32)
    t1, t2 = tr[..., :half], tr[..., half:]
    rot = jnp.concatenate([t1 * cos - t2 * sin, t1 * sin + t2 * cos], -1)
    return jnp.concatenate([rot.astype(t.dtype), t[..., ROT_DIM:]], -1)


def shared_kv(x, w_kv, pos):
    B, T, _ = x.shape
    kv = x @ w_kv
    k = kv[..., :D_MODEL].reshape(B, T, DIFF_HEADS, 2, DIFF_DH)
    v = kv[..., D_MODEL:].reshape(B, T, DIFF_HEADS, 2 * DIFF_DH)
    return rope_partial(k, pos), v


def diff_core(q, kf, vf, lam, mask):
    s = jnp.einsum('bqhcd,bkhcd->bhcqk', q.astype(F32), kf) * (DIFF_DH ** -0.5)
    if mask is not None:
        s = jnp.where(mask[None, None, None], s, -jnp.inf)
    p = jax.nn.softmax(s, axis=-1)
    attn = p[:, :, 0] - lam * p[:, :, 1]
    return jnp.einsum('bhqk,bkhe->bqhe', attn, vf)


def diff_attention(x, k_all, v_all, pos, w_q, lam_params, subln_g, w_o, layer_idx, prompt_mode):
    B, T, _ = x.shape
    q = rope_partial((x @ w_q).reshape(B, T, DIFF_HEADS, 2, DIFF_DH), pos)
    lam_init = 0.8 - 0.6 * math.exp(-0.3 * layer_idx)
    lp = lam_params.astype(F32)
    lam = jnp.exp(jnp.sum(lp[0] * lp[1])) - jnp.exp(jnp.sum(lp[2] * lp[3])) + lam_init
    kf, vf = k_all.astype(F32), v_all.astype(F32)
    if prompt_mode:
        nb = T // Q_BLOCK
        qb = q.reshape(B, nb, Q_BLOCK, DIFF_HEADS, 2, DIFF_DH).transpose(1, 0, 2, 3, 4, 5)
        k_chunk = jnp.arange(kf.shape[1]) // CHUNK

        def block(args):
            qi, bi = args
            q_chunk = (bi * Q_BLOCK + jnp.arange(Q_BLOCK)) // CHUNK
            mask = k_chunk[None, :] <= q_chunk[:, None]
            return diff_core(qi, kf, vf, lam, mask)

        o = lax.map(block, (qb, jnp.arange(nb)))
        o = o.transpose(1, 0, 2, 3, 4).reshape(B, T, DIFF_HEADS, 2 * DIFF_DH)
    else:
        o = diff_core(q, kf, vf, lam, None)
    o = o * lax.rsqrt(jnp.mean(o * o, -1, keepdims=True) + SUBLN_EPS) * subln_g.astype(F32)
    o = o * (1.0 - lam_init)
    return o.reshape(B, T, D_MODEL).astype(x.dtype) @ w_o


def peer_rows(xr, w_query, sub_keys, expert_u, expert_v):
    n = xr.shape[0]
    q = (xr @ w_query).reshape(n, PEER_HEADS, 2, PEER_DHALF).astype(F32)
    s = jnp.einsum('nhcd,ckd->nhck', q, sub_keys.astype(F32))
    s1, i1 = lax.top_k(s[:, :, 0], PEER_TOPK)
    s2, i2 = lax.top_k(s[:, :, 1], PEER_TOPK)
    cand_s = (s1[..., :, None] + s2[..., None, :]).reshape(n, PEER_HEADS, PEER_TOPK * PEER_TOPK)
    cand_i = (i1[..., :, None] * PEER_KEYS + i2[..., None, :]).reshape(n, PEER_HEADS, PEER_TOPK * PEER_TOPK)
    top_s, top_pos = lax.top_k(cand_s, PEER_TOPK)
    eidx = jnp.take_along_axis(cand_i, top_pos, axis=-1)
    gate = jax.nn.softmax(top_s, axis=-1)
    u = expert_u[eidx].astype(F32)
    h = jax.nn.gelu(jnp.einsum('nhkd,nd->nhk', u, xr.astype(F32)), approximate=False)
    out = jnp.einsum('nhk,nhkd->nd', gate * h, expert_v[eidx].astype(F32))
    return out.astype(xr.dtype)


def peer_ffn(x, w_query, sub_keys, expert_u, expert_v):
    B, T, D = x.shape
    n = B * T
    nb = -(-n // PEER_ROWS)
    rows = jnp.pad(x.reshape(n, D), ((0, nb * PEER_ROWS - n), (0, 0)))
    out = lax.map(lambda r: peer_rows(r, w_query, sub_keys, expert_u, expert_v),
                  rows.reshape(nb, PEER_ROWS, D))
    return out.reshape(nb * PEER_ROWS, D)[:n].reshape(B, T, D)


def _trunk(x, pos, shift_prev, wkv_prev, past_k, past_v, W):
    new_shift, new_wkv = [], []
    k_all = v_all = k_new = v_new = None
    for l in range(DEPTH):
        if l < N_A_LAYERS:
            mix, sh, st = rwkv7_time_mix(
                x, shift_prev[l], wkv_prev[l], W['rwkv_mu'][l], W['rwkv_w_rkv'][l],
                W['rwkv_w0'][l], W['rwkv_w1'][l], W['rwkv_w2'][l],
                W['rwkv_a0'][l], W['rwkv_a1'][l], W['rwkv_a2'][l],
                W['rwkv_g1'][l], W['rwkv_g2'][l], W['rwkv_k_k'][l], W['rwkv_k_a'][l],
                W['rwkv_r_k'][l], W['rwkv_lnx_g'][l], W['rwkv_lnx_b'][l], W['rwkv_w_out'][l])
            new_shift.append(sh)
            new_wkv.append(st)
        else:
            if l == N_A_LAYERS:
                k_new, v_new = shared_kv(x, W['w_kv'], pos)
                if past_k is None:
                    k_all, v_all = k_new, v_new
                else:
                    k_all = jnp.concatenate([past_k.astype(k_new.dtype), k_new], axis=1)
                    v_all = jnp.concatenate([past_v.astype(v_new.dtype), v_new], axis=1)
            j = l - N_A_LAYERS
            mix = diff_attention(x, k_all, v_all, pos, W['diff_w_q'][j], W['diff_lambda'][j],
                                 W['diff_subln_g'][j], W['diff_w_o'][j], l, past_k is None)
        x = layer_norm(DN_ALPHA * x + mix, W['ln_mix_g'][l], W['ln_mix_b'][l])
        ffn = peer_ffn(x, W['peer_w_query'][l], W['peer_sub_keys'][l], W['peer_u'][l], W['peer_v'][l])
        x = layer_norm(DN_ALPHA * x + ffn, W['ln_ffn_g'][l], W['ln_ffn_b'][l])
    return x, jnp.stack(new_shift), jnp.stack(new_wkv), k_new, v_new


def setup_inputs(seed: int = 0) -> dict:
    key = jax.random.key(seed)
    ks = iter(jax.random.split(key, 64))
    D, H, N = D_MODEL, RWKV_HEADS, RWKV_HEAD
    A, Bn = N_A_LAYERS, N_B_LAYERS

    def nrm(shape, scale):
        return jax.random.normal(next(ks), shape, F32) * scale

    def uni(shape, lo, hi):
        return jax.random.uniform(next(ks), shape, F32, minval=lo, maxval=hi)

    return {
        'x_prompt': nrm((BATCH, SEQ, D), 1.0),
        'x_sample': nrm((DEC_BATCH, DEC_SEQ, D), 1.0),
        'state_wkv': nrm((A, DEC_BATCH, H, N, N), 0.1),
        'state_shift': nrm((A, DEC_BATCH, 1, D), 1.0),
        'cache_k': nrm((DEC_BATCH, PAST_LEN, DIFF_HEADS, 2, DIFF_DH), 1.0),
        'cache_v': nrm((DEC_BATCH, PAST_LEN, DIFF_HEADS, 2 * DIFF_DH), 1.0),
        'rwkv_mu': uni((A, 6, D), 0.0, 1.0),
        'rwkv_w_rkv': nrm((A, 3, D, D), D ** -0.5),
        'rwkv_w0': uni((A, D), -4.0, 1.0),
        'rwkv_w1': nrm((A, D, DECAY_LORA), D ** -0.5),
        'rwkv_w2': nrm((A, DECAY_LORA, D), 0.1 * DECAY_LORA ** -0.5),
        'rwkv_a0': nrm((A, D), 0.1),
        'rwkv_a1': nrm((A, D, AAA_LORA), D ** -0.5),
        'rwkv_a2': nrm((A, AAA_LORA, D), AAA_LORA ** -0.5),
        'rwkv_g1': nrm((A, D, GATE_LORA), D ** -0.5),
        'rwkv_g2': nrm((A, GATE_LORA, D), GATE_LORA ** -0.5),
        'rwkv_k_k': 0.85 + nrm((A, D), 0.05),
        'rwkv_k_a': 1.0 + nrm((A, D), 0.05),
        'rwkv_r_k': nrm((A, H, N), 0.1),
        'rwkv_lnx_g': 1.0 + nrm((A, D), 0.05),
        'rwkv_lnx_b': nrm((A, D), 0.02),
        'rwkv_w_out': nrm((A, D, D), D ** -0.5 * DN_BETA),
        'w_kv': nrm((D, 2 * D), D ** -0.5),
        'diff_w_q': nrm((Bn, D, D), D ** -0.5),
        'diff_lambda': nrm((Bn, 4, DIFF_DH), 0.1),
        'diff_subln_g': 1.0 + nrm((Bn, 2 * DIFF_DH), 0.05),
        'diff_w_o': nrm((Bn, D, D), D ** -0.5 * DN_BETA),
        'ln_mix_g': 1.0 + nrm((DEPTH, D), 0.05),
        'ln_mix_b': nrm((DEPTH, D), 0.02),
        'ln_ffn_g': 1.0 + nrm((DEPTH, D), 0.05),
        'ln_ffn_b': nrm((DEPTH, D), 0.02),
        'peer_w_query': nrm((DEPTH, D, PEER_HEADS * PEER_DKEY), D ** -0.5),
        'peer_sub_keys': nrm((DEPTH, 2, PEER_KEYS, PEER_DHALF), PEER_DHALF ** -0.5),
        'peer_u': nrm((DEPTH, PEER_EXPERTS, D), D ** -0.5),
        'peer_v': nrm((DEPTH, PEER_EXPERTS, D), DN_BETA * PEER_HEADS ** -0.5),
    }


def reference(x_prompt, x_sample, state_wkv, state_shift, cache_k, cache_v,
              rwkv_mu, rwkv_w_rkv, rwkv_w0, rwkv_w1, rwkv_w2, rwkv_a0, rwkv_a1, rwkv_a2,
              rwkv_g1, rwkv_g2, rwkv_k_k, rwkv_k_a, rwkv_r_k, rwkv_lnx_g, rwkv_lnx_b, rwkv_w_out,
              w_kv, diff_w_q, diff_lambda, diff_subln_g, diff_w_o,
              ln_mix_g, ln_mix_b, ln_ffn_g, ln_ffn_b,
              peer_w_query, peer_sub_keys, peer_u, peer_v):
    W = dict(rwkv_mu=rwkv_mu, rwkv_w_rkv=rwkv_w_rkv, rwkv_w0=rwkv_w0, rwkv_w1=rwkv_w1,
             rwkv_w2=rwkv_w2, rwkv_a0=rwkv_a0, rwkv_a1=rwkv_a1, rwkv_a2=rwkv_a2,
             rwkv_g1=rwkv_g1, rwkv_g2=rwkv_g2, rwkv_k_k=rwkv_k_k, rwkv_k_a=rwkv_k_a,
             rwkv_r_k=rwkv_r_k, rwkv_lnx_g=rwkv_lnx_g, rwkv_lnx_b=rwkv_lnx_b,
             rwkv_w_out=rwkv_w_out, w_kv=w_kv, diff_w_q=diff_w_q, diff_lambda=diff_lambda,
             diff_subln_g=diff_subln_g, diff_w_o=diff_w_o, ln_mix_g=ln_mix_g,
             ln_mix_b=ln_mix_b, ln_ffn_g=ln_ffn_g, ln_ffn_b=ln_ffn_b,
             peer_w_query=peer_w_query, peer_sub_keys=peer_sub_keys,
             peer_u=peer_u, peer_v=peer_v)
    Bp, Tp, D = x_prompt.shape
    _, Ts, _ = x_sample.shape
    pos_p = jnp.arange(Tp, dtype=jnp.int32)
    shift0 = jnp.zeros((N_A_LAYERS, Bp, 1, D), x_prompt.dtype)
    wkv0 = jnp.zeros((N_A_LAYERS, Bp, RWKV_HEADS, RWKV_HEAD, RWKV_HEAD), x_prompt.dtype)
    y_prompt, state_shift_prompt, state_wkv_prompt, cache_k_prompt, cache_v_prompt = _trunk(
        x_prompt, pos_p, shift0, wkv0, None, None, W)
    pos_s = PAST_LEN + jnp.arange(Ts, dtype=jnp.int32)
    y_sample, state_shift_sample, state_wkv_sample, cache_k_sample, cache_v_sample = _trunk(
        x_sample, pos_s, state_shift, state_wkv, cache_k, cache_v, W)
    return (y_prompt, y_sample, state_wkv_prompt, state_shift_prompt, cache_k_prompt, cache_v_prompt,
            state_wkv_sample, state_shift_sample, cache_k_sample, cache_v_sample)
```

```python
import functools
import math

import jax
import jax.numpy as jnp
from jax import lax
from jax.experimental import pallas as pl
from jax.experimental.pallas import tpu as pltpu

F32 = jnp.float32
BF16 = jnp.bfloat16

D_MODEL = 1024
DEPTH = 2
CHUNK = 64
RWKV_HEAD = 64
RWKV_HEADS = D_MODEL // RWKV_HEAD
GN_EPS = 64e-5
DIFF_HEADS = 8
DIFF_DH = D_MODEL // (2 * DIFF_HEADS)
ROT_DIM = DIFF_DH // 4
ROPE_THETA = 500000.0
SUBLN_EPS = 1e-5
PEER_HEADS = 8
PEER_KEYS = 128
PEER_DHALF = 128
PEER_TOPK = 16
DN_ALPHA = (2.0 * DEPTH) ** 0.25
LN_EPS = 1e-5
DECAY_SCALE = math.exp(-0.5)
SQRT_HALF = math.sqrt(0.5)

LANES = 128
MIB = 1 << 20


def _cparams(semantics, vmem_mib):
    return pltpu.CompilerParams(dimension_semantics=semantics, vmem_limit_bytes=vmem_mib * MIB)


def _dot(a, b):
    return jnp.dot(a, b, preferred_element_type=F32)


def _dot_nt(a, b):
    return lax.dot_general(a, b, (((1,), (1,)), ((), ())), preferred_element_type=F32)


def _split_bf16(x):
    hi = x.astype(BF16)
    lo = (x - hi.astype(F32)).astype(BF16)
    return hi, lo


def _seg_sum(x, seg_ref, segt_ref):
    hi, lo = _split_bf16(x)
    s = _dot(hi, seg_ref[...]) + _dot(lo, seg_ref[...])
    shi, slo = _split_bf16(s)
    return _dot(shi, segt_ref[...]) + _dot(slo, segt_ref[...])


def _layer_norm(x, g, b):
    mu = jnp.mean(x, axis=-1, keepdims=True)
    xc = x - mu
    var = jnp.mean(xc * xc, axis=-1, keepdims=True)
    return xc * lax.rsqrt(var + LN_EPS) * g + b


def _full(shape):
    nd = len(shape)
    return pl.BlockSpec(shape, lambda *_: (0,) * nd)


def _rwkv_pre_kernel(x_ref, xp_ref, mu_ref, wr_ref, wk_ref, wv_ref, w1_ref, w2_ref, w0_ref,
                     a1_ref, a2_ref, a0_ref, g1_ref, g2_ref, kk_ref, ka_ref, seg_ref, segt_ref,
                     r_o, w_o, k_o, v_o, kk_o, b_o, g_o):
    x = x_ref[...]
    xx = xp_ref[...] - x

    def mix(i):
        return (x + xx * mu_ref[i:i + 1, :]).astype(BF16)

    r = _dot(mix(0), wr_ref[...])
    k = _dot(mix(1), wk_ref[...])
    v = _dot(mix(2), wv_ref[...])
    z = w0_ref[...] + _dot(jnp.tanh(_dot(mix(3), w1_ref[...])).astype(BF16), w2_ref[...])
    w = jnp.exp(-DECAY_SCALE * jax.nn.sigmoid(z))
    a = jax.nn.sigmoid(a0_ref[...] + _dot(_dot(mix(4), a1_ref[...]).astype(BF16), a2_ref[...]))
    g = _dot(jax.nn.sigmoid(_dot(mix(5), g1_ref[...])).astype(BF16), g2_ref[...])
    kk = k * kk_ref[...]
    nrm = jnp.sqrt(_seg_sum(kk * kk, seg_ref, segt_ref))
    kk = kk / jnp.maximum(nrm, 1e-12)
    r_o[...] = r
    w_o[...] = w
    k_o[...] = k * (1.0 + (a - 1.0) * ka_ref[...])
    v_o[...] = v
    kk_o[...] = kk
    b_o[...] = kk * a
    g_o[...] = g


def _rwkv_pre(x, xp, P, tn):
    n = x.shape[0]
    row = pl.BlockSpec((tn, D_MODEL), lambda i: (i, 0))
    consts = [P['mu'], P['wr'], P['wk'], P['wv'], P['w1'], P['w2'], P['w0'], P['a1'], P['a2'], P['a0'],
              P['g1'], P['g2'], P['k_k'], P['k_a'], P['seg'], P['segt']]
    out = jax.ShapeDtypeStruct((n, D_MODEL), F32)
    return pl.pallas_call(
        _rwkv_pre_kernel,
        grid=(n // tn,),
        in_specs=[row, row] + [_full(c.shape) for c in consts],
        out_specs=[row] * 7,
        out_shape=[out] * 7,
        compiler_params=_cparams(("parallel",), 48),
        name="rwkv_pre",
    )(x, xp, *consts)


def _rwkv_scan_kernel(r_ref, w_ref, k_ref, v_ref, kk_ref, b_ref, s0_ref, y_ref, s_ref, *, tc):
    @pl.when(pl.program_id(1) == 0)
    def _():
        s_ref[...] = s0_ref[...]

    def step(t, carry):
        kk_t = kk_ref[t]
        b_t = b_ref[t]
        k_t = k_ref[t]
        w_t = w_ref[t]
        r_t = r_ref[t]

        def vrow(vi, c):
            s = s_ref[vi]
            skk = jnp.sum(s * kk_t, axis=0, keepdims=True)
            vv = v_ref[t, pl.ds(vi, 1), :]
            s = s * w_t - skk * b_t + vv * k_t
            s_ref[vi] = s
            y_ref[t, pl.ds(vi, 1), :] = jnp.sum(s * r_t, axis=0, keepdims=True)
            return c

        return lax.fori_loop(0, RWKV_HEAD, vrow, carry, unroll=4)

    lax.fori_loop(0, tc, step, 0)


def _rwkv_scan(r, w, k, v, kk, b, s0, tc):
    T, N, L = r.shape
    seq = pl.BlockSpec((tc, N, LANES), lambda g, i: (i, 0, g))
    st = pl.BlockSpec((N, N, LANES), lambda g, i: (0, 0, g))
    return pl.pallas_call(
        functools.partial(_rwkv_scan_kernel, tc=tc),
        grid=(L // LANES, T // tc),
        in_specs=[seq] * 6 + [st],
        out_specs=[seq, st],
        out_shape=[jax.ShapeDtypeStruct((T, N, L), F32), jax.ShapeDtypeStruct((N, N, L), F32)],
        compiler_params=_cparams(("parallel", "arbitrary"), 48),
        name="rwkv_scan",
    )(r, w, k, v, kk, b, s0)


def _rwkv_post_kernel(y_ref, r_ref, k_ref, v_ref, g_ref, x_ref, rk_ref, lg_ref, lb_ref, wo_ref,
                      ng_ref, nb_ref, seg_ref, segt_ref, x_o, xt_o):
    y = y_ref[...]
    inv_n = 1.0 / RWKV_HEAD
    yc = y - _seg_sum(y, seg_ref, segt_ref) * inv_n
    var = _seg_sum(yc * yc, seg_ref, segt_ref) * inv_n
    yn = yc * lax.rsqrt(var + GN_EPS) * lg_ref[...] + lb_ref[...]
    bonus = _seg_sum(r_ref[...] * k_ref[...] * rk_ref[...], seg_ref, segt_ref) * v_ref[...]
    o = _dot(((yn + bonus) * g_ref[...]).astype(BF16), wo_ref[...])
    xn = _layer_norm(DN_ALPHA * x_ref[...] + o, ng_ref[...], nb_ref[...])
    x_o[...] = xn
    xt_o[...] = xn.T.astype(BF16)


def _rwkv_post(y, r, k, v, g, x, P, tn):
    n = x.shape[0]
    row = pl.BlockSpec((tn, D_MODEL), lambda i: (i, 0))
    consts = [P['r_k'], P['lnx_g'], P['lnx_b'], P['w_out'], P['ln_mix_g0'], P['ln_mix_b0'], P['seg'], P['segt']]
    return pl.pallas_call(
        _rwkv_post_kernel,
        grid=(n // tn,),
        in_specs=[row] * 6 + [_full(c.shape) for c in consts],
        out_specs=[row, pl.BlockSpec((D_MODEL, tn), lambda i: (0, i))],
        out_shape=[jax.ShapeDtypeStruct((n, D_MODEL), F32), jax.ShapeDtypeStruct((D_MODEL, n), BF16)],
        compiler_params=_cparams(("parallel",), 48),
        name="rwkv_post",
    )(y, r, k, v, g, x, *consts)


def _top_values(s, count):
    vals = []
    cur = s
    for _ in range(count):
        m = jnp.max(cur, axis=0, keepdims=True)
        vals.append(m)
        cur = jnp.where(cur == m, -jnp.inf, cur)
    return vals


def _peer_select_kernel(xt_ref, wq_ref, sk_ref, s1_o, e1_o, s2_o, e2_o, tau_o):
    qt = _dot(wq_ref[...], xt_ref[...])
    tn = qt.shape[1]
    row8 = lax.broadcasted_iota(jnp.int32, (8, tn), 0)
    taus = []
    for h in range(PEER_HEADS):
        q1 = qt[(2 * h) * PEER_DHALF:(2 * h + 1) * PEER_DHALF].astype(BF16)
        q2 = qt[(2 * h + 1) * PEER_DHALF:(2 * h + 2) * PEER_DHALF].astype(BF16)
        s1 = _dot(sk_ref[0], q1)
        s2 = _dot(sk_ref[1], q2)
        v1 = _top_values(s1, PEER_TOPK)
        v2 = _top_values(s2, PEER_TOPK)
        v1a = jnp.concatenate(v1, axis=0)
        cands = [v1a[0:8] + v2[0], v1a[8:16] + v2[0]]
        for bi in range(1, PEER_TOPK):
            cnt = PEER_TOPK // (bi + 1)
            cands.append(jnp.where(row8 < cnt, v1a[0:8] + v2[bi], -jnp.inf))
        tops = []
        for _ in range(PEER_TOPK):
            m = cands[0]
            for c in cands[1:]:
                m = jnp.maximum(m, c)
            m = jnp.max(m, axis=0, keepdims=True)
            tops.append(m)
            cands = [jnp.where(c == m, -jnp.inf, c) for c in cands]
        z = jnp.zeros_like(tops[0])
        for t in tops:
            z = z + jnp.exp(t - tops[0])
        s1_o[h] = s1
        e1_o[h] = jnp.exp(s1 - v1[0])
        s2_o[h] = s2
        e2_o[h] = jnp.exp(s2 - v2[0]) / z
        taus.append(tops[PEER_TOPK - 1])
    tau_o[...] = jnp.concatenate(taus, axis=0)


def _peer_select(xt, wqt, sk, tn):
    n = xt.shape[1]
    per_head = pl.BlockSpec((PEER_HEADS, PEER_KEYS, tn), lambda i: (0, 0, i))
    big = jax.ShapeDtypeStruct((PEER_HEADS, PEER_KEYS, n), F32)
    return pl.pallas_call(
        _peer_select_kernel,
        grid=(n // tn,),
        in_specs=[pl.BlockSpec((D_MODEL, tn), lambda i: (0, i)), _full(wqt.shape), _full(sk.shape)],
        out_specs=[per_head] * 4 + [pl.BlockSpec((PEER_HEADS, tn), lambda i: (0, i))],
        out_shape=[big] * 4 + [jax.ShapeDtypeStruct((PEER_HEADS, n), F32)],
        compiler_params=_cparams(("parallel",), 48),
        name="peer_select",
    )(xt, wqt, sk)


PEER_ROWS_PER_BLOCK = 8


def _peer_dense_kernel(xt_ref, u_ref, vt_ref, s1_ref, e1_ref, s2_ref, e2_ref, tau_ref, x_ref,
                       ng_ref, nb_ref, x_o, g_scr, h_scr, acc_scr, *, tn):
    e = pl.program_id(1)

    @pl.when(e == 0)
    def _():
        acc_scr[...] = jnp.zeros_like(acc_scr)

    g_scr[...] = _dot(u_ref[...], xt_ref[...])

    def lane_chunk(lc, carry):
        l0 = pl.multiple_of(lc * LANES, LANES)
        cols = pl.ds(l0, LANES)
        for ii in range(PEER_ROWS_PER_BLOCK):
            rows = slice(ii * PEER_KEYS, (ii + 1) * PEER_KEYS)
            coef = jnp.zeros((PEER_KEYS, LANES), F32)
            for h in range(PEER_HEADS):
                zsum = s1_ref[h, ii:ii + 1, cols] + s2_ref[h, :, cols]
                gate = e1_ref[h, ii:ii + 1, cols] * e2_ref[h, :, cols]
                coef = coef + jnp.where(zsum >= tau_ref[h:h + 1, cols], gate, 0.0)
            gv = g_scr[rows, cols]
            act = 0.5 * gv * (1.0 + lax.erf(gv * SQRT_HALF))
            h_scr[rows, cols] = (act * coef).astype(BF16)
        return carry

    lax.fori_loop(0, tn // LANES, lane_chunk, 0)
    acc_scr[...] += _dot(vt_ref[...], h_scr[...])

    @pl.when(e == pl.num_programs(1) - 1)
    def _():
        ffn = acc_scr[...].T
        x_o[...] = _layer_norm(DN_ALPHA * x_ref[...] + ffn, ng_ref[...], nb_ref[...])


def _peer_dense(xt, u, vt, s1, e1, s2, e2, tau, x, ng, nb, tn):
    n = x.shape[0]
    eb = PEER_ROWS_PER_BLOCK * PEER_KEYS
    n_exp = u.shape[0]
    rows1 = pl.BlockSpec((PEER_HEADS, PEER_ROWS_PER_BLOCK, tn), lambda t, e: (0, e, t))
    rows2 = pl.BlockSpec((PEER_HEADS, PEER_KEYS, tn), lambda t, e: (0, 0, t))
    return pl.pallas_call(
        functools.partial(_peer_dense_kernel, tn=tn),
        grid=(n // tn, n_exp // eb),
        in_specs=[pl.BlockSpec((D_MODEL, tn), lambda t, e: (0, t)),
                  pl.BlockSpec((eb, D_MODEL), lambda t, e: (e, 0)),
                  pl.BlockSpec((D_MODEL, eb), lambda t, e: (0, e)),
                  rows1, rows1, rows2, rows2,
                  pl.BlockSpec((PEER_HEADS, tn), lambda t, e: (0, t)),
                  pl.BlockSpec((tn, D_MODEL), lambda t, e: (t, 0)),
                  _full(ng.shape), _full(nb.shape)],
        out_specs=pl.BlockSpec((tn, D_MODEL), lambda t, e: (t, 0)),
        out_shape=jax.ShapeDtypeStruct((n, D_MODEL), F32),
        scratch_shapes=[pltpu.VMEM((eb, tn), F32), pltpu.VMEM((eb, tn), BF16), pltpu.VMEM((D_MODEL, tn), F32)],
        compiler_params=_cparams(("parallel", "arbitrary"), 56),
        name="peer_dense",
    )(xt, u, vt, s1, e1, s2, e2, tau, x, ng, nb)


def _peer_ffn(x, xt, L, tn_sel, tn_dense):
    s1, e1, s2, e2, tau = _peer_select(xt, L['wqt'], L['sk'], tn_sel)
    return _peer_dense(xt, L['u'], L['vt'], s1, e1, s2, e2, tau, x, L['ln_ffn_g'], L['ln_ffn_b'], tn_dense)


def _rope(t, c_ref, sa_ref, sb_ref):
    half = ROT_DIM // 2
    up = pltpu.roll(t, D_MODEL - half, 1)
    down = pltpu.roll(t, half, 1)
    return t * c_ref[...] + up * sa_ref[...] + down * sb_ref[...]


def _kvq_kernel(x_ref, wkv_ref, wq_ref, c_ref, sa_ref, sb_ref, k_o, v_o, qb_o, kb_o, vb_o):
    x = x_ref[0].astype(BF16)
    kv = _dot(x, wkv_ref[...])
    k = _rope(kv[:, :D_MODEL], c_ref, sa_ref, sb_ref)
    v = kv[:, D_MODEL:]
    q = _rope(_dot(x, wq_ref[...]), c_ref, sa_ref, sb_ref)
    k_o[0] = k
    v_o[0] = v
    qb_o[0] = (q * (DIFF_DH ** -0.5)).astype(BF16)
    kb_o[0] = k.astype(BF16)
    vb_o[0] = v.astype(BF16)


def _kvq_proj(x, wkv, wq, rc, rsa, rsb, tt):
    B, T, _ = x.shape
    blk = pl.BlockSpec((1, tt, D_MODEL), lambda b, j: (b, j, 0))
    tab = pl.BlockSpec((tt, D_MODEL), lambda b, j: (j, 0))
    f = jax.ShapeDtypeStruct((B, T, D_MODEL), F32)
    h = jax.ShapeDtypeStruct((B, T, D_MODEL), BF16)
    return pl.pallas_call(
        _kvq_kernel,
        grid=(B, T // tt),
        in_specs=[blk, _full(wkv.shape), _full(wq.shape), tab, tab, tab],
        out_specs=[blk] * 5,
        out_shape=[f, f, h, h, h],
        compiler_params=_cparams(("parallel", "parallel"), 48),
        name="kvq_proj",
    )(x, wkv, wq, rc, rsa, rsb)


def _bcast_cols(m, width):
    if width <= LANES:
        return m[:, :width]
    return jnp.concatenate([m] * (width // LANES), axis=1)


def _attn_update(q, k, v, m_ref, l_ref, acc_ref, idx, mask):
    lane = lax.broadcasted_iota(jnp.int32, q.shape, 1)
    zero = jnp.zeros_like(q)
    for c in range(2):
        qc = jnp.where(lane < DIFF_DH, q, zero) if c == 0 else jnp.where(lane >= DIFF_DH, q, zero)
        s = _dot_nt(qc, k)
        if mask is not None:
            s = jnp.where(mask, s, -jnp.inf)
        m_prev = m_ref[idx + c]
        m_new = jnp.maximum(m_prev, jnp.max(s, axis=1, keepdims=True))
        p = jnp.exp(s - _bcast_cols(m_new, s.shape[1]))
        alpha = jnp.exp(m_prev - m_new)
        l_ref[idx + c] = alpha * l_ref[idx + c] + jnp.sum(p, axis=1, keepdims=True)
        acc_ref[idx + c] = alpha * acc_ref[idx + c] + _dot(p.astype(BF16), v)
        m_ref[idx + c] = m_new


def _attn_init(m_ref, l_ref, acc_ref):
    m_ref[...] = jnp.full_like(m_ref, -jnp.inf)
    l_ref[...] = jnp.zeros_like(l_ref)
    acc_ref[...] = jnp.zeros_like(acc_ref)


def _attn_finish(lam, sg, m_ref, l_ref, acc_ref, idx, lam_init):
    o = acc_ref[idx] / l_ref[idx] - lam * (acc_ref[idx + 1] / l_ref[idx + 1])
    o = o * lax.rsqrt(jnp.mean(o * o, axis=-1, keepdims=True) + SUBLN_EPS) * sg
    return (o * (1.0 - lam_init)).astype(BF16)


def _attn_prompt_kernel(lam_ref, q_ref, k_ref, v_ref, sg_ref, o_ref, m_ref, l_ref, acc_ref, *, tq, lam_init):
    qi = pl.program_id(2)
    ki = pl.program_id(3)

    @pl.when(ki == 0)
    def _():
        _attn_init(m_ref, l_ref, acc_ref)

    @pl.when(ki < qi)
    def _():
        _attn_update(q_ref[0], k_ref[0], v_ref[0], m_ref, l_ref, acc_ref, 0, None)

    @pl.when(ki == qi)
    def _():
        qc = lax.broadcasted_iota(jnp.int32, (tq, tq), 0) // CHUNK
        kc = lax.broadcasted_iota(jnp.int32, (tq, tq), 1) // CHUNK
        _attn_update(q_ref[0], k_ref[0], v_ref[0], m_ref, l_ref, acc_ref, 0, kc <= qc)
        o_ref[0] = _attn_finish(lam_ref[0], sg_ref[...], m_ref, l_ref, acc_ref, 0, lam_init)


def _attn_prompt(lam, q, k, v, sg, tq, lam_init):
    B, T, _ = q.shape
    nq = T // tq
    qspec = pl.BlockSpec((1, tq, LANES), lambda b, h, i, j: (b, i, h))
    kspec = pl.BlockSpec((1, tq, LANES), lambda b, h, i, j: (b, jnp.minimum(i, j), h))
    return pl.pallas_call(
        functools.partial(_attn_prompt_kernel, tq=tq, lam_init=lam_init),
        grid=(B, DIFF_HEADS, nq, nq),
        in_specs=[pl.BlockSpec(memory_space=pltpu.SMEM), qspec, kspec, kspec, _full(sg.shape)],
        out_specs=qspec,
        out_shape=jax.ShapeDtypeStruct((B, T, D_MODEL), BF16),
        scratch_shapes=[pltpu.VMEM((2, tq, LANES), F32)] * 3,
        compiler_params=_cparams(("parallel", "parallel", "parallel", "arbitrary"), 32),
        name="attn_prompt",
    )(lam, q, k, v, sg)


def _attn_sample_kernel(lam_ref, q_ref, ck_ref, cv_ref, nk_ref, nv_ref, sg_ref, o_ref,
                        m_ref, l_ref, acc_ref, *, lam_init):
    j = pl.program_id(1)

    @pl.when(j == 0)
    def _():
        _attn_init(m_ref, l_ref, acc_ref)

    for h in range(DIFF_HEADS):
        cols = slice(h * LANES, (h + 1) * LANES)
        _attn_update(q_ref[0, :, cols], ck_ref[0, :, cols].astype(BF16), cv_ref[0, :, cols].astype(BF16),
                     m_ref, l_ref, acc_ref, 2 * h, None)

    @pl.when(j == pl.num_programs(1) - 1)
    def _():
        for h in range(DIFF_HEADS):
            cols = slice(h * LANES, (h + 1) * LANES)
            _attn_update(q_ref[0, :, cols], nk_ref[0, :, cols], nv_ref[0, :, cols],
                         m_ref, l_ref, acc_ref, 2 * h, None)
            o_ref[0, :, cols] = _attn_finish(lam_ref[0], sg_ref[...], m_ref, l_ref, acc_ref, 2 * h, lam_init)


def _attn_sample(lam, q, ck, cv, nk, nv, sg, tk, lam_init):
    B, T, _ = q.shape
    past = ck.shape[1]
    new = pl.BlockSpec((1, T, D_MODEL), lambda b, j: (b, 0, 0))
    old = pl.BlockSpec((1, tk, D_MODEL), lambda b, j: (b, j, 0))
    return pl.pallas_call(
        functools.partial(_attn_sample_kernel, lam_init=lam_init),
        grid=(B, past // tk),
        in_specs=[pl.BlockSpec(memory_space=pltpu.SMEM), new, old, old, new, new, _full(sg.shape)],
        out_specs=new,
        out_shape=jax.ShapeDtypeStruct((B, T, D_MODEL), BF16),
        scratch_shapes=[pltpu.VMEM((2 * DIFF_HEADS, T, LANES), F32)] * 3,
        compiler_params=_cparams(("parallel", "arbitrary"), 48),
        name="attn_sample",
    )(lam, q, ck, cv, nk, nv, sg)


def _attn_out_kernel(o_ref, x_ref, wo_ref, ng_ref, nb_ref, x_o, xt_o):
    mixv = _dot(o_ref[...], wo_ref[...])
    xn = _layer_norm(DN_ALPHA * x_ref[...] + mixv, ng_ref[...], nb_ref[...])
    x_o[...] = xn
    xt_o[...] = xn.T.astype(BF16)


def _attn_out(o, x, wo, ng, nb, tn):
    n = x.shape[0]
    row = pl.BlockSpec((tn, D_MODEL), lambda i: (i, 0))
    return pl.pallas_call(
        _attn_out_kernel,
        grid=(n // tn,),
        in_specs=[row, row, _full(wo.shape), _full(ng.shape), _full(nb.shape)],
        out_specs=[row, pl.BlockSpec((D_MODEL, tn), lambda i: (0, i))],
        out_shape=[jax.ShapeDtypeStruct((n, D_MODEL), F32), jax.ShapeDtypeStruct((D_MODEL, n), BF16)],
        compiler_params=_cparams(("parallel",), 32),
        name="attn_out",
    )(o, x, wo, ng, nb)


def _rope_tables(pos):
    half = ROT_DIM // 2
    inv = jnp.power(ROPE_THETA, -jnp.arange(half, dtype=F32) * 2.0 / ROT_DIM)
    ang = pos.astype(F32)[:, None] * inv[None, :]
    cos, sin = jnp.cos(ang), jnp.sin(ang)
    T = pos.shape[0]
    rest = DIFF_DH - ROT_DIM
    c = jnp.concatenate([cos, cos, jnp.ones((T, rest), F32)], axis=1)
    sa = jnp.concatenate([-sin, jnp.zeros((T, half + rest), F32)], axis=1)
    sb = jnp.concatenate([jnp.zeros((T, half), F32), sin, jnp.zeros((T, rest), F32)], axis=1)
    reps = D_MODEL // DIFF_DH
    return jnp.tile(c, (1, reps)), jnp.tile(sa, (1, reps)), jnp.tile(sb, (1, reps))


def _pick(n, prefs):
    for t in prefs:
        if n % t == 0:
            return t
    return n


def _trunk(x, pos, shift_prev, wkv_prev, past_k, past_v, P):
    B, T, _ = x.shape
    n = B * T
    H, N = RWKV_HEADS, RWKV_HEAD
    x2d = x.reshape(n, D_MODEL)
    tn = _pick(n, (256,))

    xprev = jnp.concatenate([shift_prev[0].astype(x.dtype), x[:, :-1]], axis=1).reshape(n, D_MODEL)
    r, w, k, v, kk, b, g = _rwkv_pre(x2d, xprev, P, tn)

    def to_scan(a):
        return a.reshape(B, T, H, N).transpose(1, 3, 0, 2).reshape(T, N, B * H)

    s0 = wkv_prev[0].astype(F32).transpose(2, 3, 0, 1).reshape(N, N, B * H)
    y, s_fin = _rwkv_scan(to_scan(r), to_scan(w), to_scan(k), to_scan(v), to_scan(kk), to_scan(b), s0,
                          _pick(T, (16,)))
    y = y.reshape(T, N, B, H).transpose(2, 0, 3, 1).reshape(n, D_MODEL)
    new_wkv = s_fin.reshape(N, N, B, H).transpose(2, 3, 0, 1)[None].astype(x.dtype)
    new_shift = x[:, -1:][None]
    x1, x1t = _rwkv_post(y, r, k, v, g, x2d, P, tn)
    tn_dense = _pick(n, (512, 256))
    x2 = _peer_ffn(x1, x1t, P['peer'][0], tn, tn_dense)

    rc, rsa, rsb = _rope_tables(pos)
    k_new, v_new, qb, kb, vb = _kvq_proj(x2.reshape(B, T, D_MODEL), P['w_kv'], P['w_q'], rc, rsa, rsb,
                                         _pick(T, (256,)))
    lam_init = 0.8 - 0.6 * math.exp(-0.3 * 1)
    if past_k is None:
        o = _attn_prompt(P['lam'], qb, kb, vb, P['subln_g'], _pick(T, (256,)), lam_init)
    else:
        past = past_k.shape[1]
        o = _attn_sample(P['lam'], qb, past_k.reshape(B, past, D_MODEL), past_v.reshape(B, past, D_MODEL),
                         kb, vb, P['subln_g'], _pick(past, (512,)), lam_init)
    x3, x3t = _attn_out(o.reshape(n, D_MODEL), x2, P['w_o'], P['ln_mix_g1'], P['ln_mix_b1'], tn)
    y_out = _peer_ffn(x3, x3t, P['peer'][1], tn, tn_dense)
    return (y_out.reshape(B, T, D_MODEL), new_shift, new_wkv,
            k_new.reshape(B, T, DIFF_HEADS, 2, DIFF_DH), v_new.reshape(B, T, DIFF_HEADS, 2 * DIFF_DH))


def kernel(x_prompt, x_sample, state_wkv, state_shift, cache_k, cache_v, rwkv_mu, rwkv_w_rkv, rwkv_w0, rwkv_w1, rwkv_w2, rwkv_a0, rwkv_a1, rwkv_a2, rwkv_g1, rwkv_g2, rwkv_k_k, rwkv_k_a, rwkv_r_k, rwkv_lnx_g, rwkv_lnx_b, rwkv_w_out, w_kv, diff_w_q, diff_lambda, diff_subln_g, diff_w_o, ln_mix_g, ln_mix_b, ln_ffn_g, ln_ffn_b, peer_w_query, peer_sub_keys, peer_u, peer_v):
    row = lambda a: a.reshape(1, -1).astype(F32)
    head_of_lane = jnp.arange(D_MODEL) // RWKV_HEAD
    seg = (head_of_lane[:, None] == jnp.arange(LANES)[None, :]).astype(BF16)
    lp = diff_lambda[0].astype(F32)
    lam_init = 0.8 - 0.6 * math.exp(-0.3 * 1)
    lam = jnp.exp(jnp.sum(lp[0] * lp[1])) - jnp.exp(jnp.sum(lp[2] * lp[3])) + lam_init
    P = dict(
        mu=rwkv_mu[0], wr=rwkv_w_rkv[0, 0].astype(BF16), wk=rwkv_w_rkv[0, 1].astype(BF16),
        wv=rwkv_w_rkv[0, 2].astype(BF16), w1=rwkv_w1[0].astype(BF16), w2=rwkv_w2[0].astype(BF16),
        w0=row(rwkv_w0[0]), a1=rwkv_a1[0].astype(BF16), a2=rwkv_a2[0].astype(BF16), a0=row(rwkv_a0[0]),
        g1=rwkv_g1[0].astype(BF16), g2=rwkv_g2[0].astype(BF16), k_k=row(rwkv_k_k[0]), k_a=row(rwkv_k_a[0]),
        r_k=row(rwkv_r_k[0]), lnx_g=row(rwkv_lnx_g[0]), lnx_b=row(rwkv_lnx_b[0]),
        w_out=rwkv_w_out[0].astype(BF16), seg=seg, segt=seg.T,
        ln_mix_g0=row(ln_mix_g[0]), ln_mix_b0=row(ln_mix_b[0]),
        ln_mix_g1=row(ln_mix_g[1]), ln_mix_b1=row(ln_mix_b[1]),
        w_kv=w_kv.astype(BF16), w_q=diff_w_q[0].astype(BF16), w_o=diff_w_o[0].astype(BF16),
        subln_g=row(diff_subln_g[0]), lam=lam.reshape(1).astype(F32),
        peer=[dict(wqt=peer_w_query[l].T.astype(BF16), sk=peer_sub_keys[l].astype(BF16),
                   u=peer_u[l].astype(BF16), vt=peer_v[l].T.astype(BF16),
                   ln_ffn_g=row(ln_ffn_g[l]), ln_ffn_b=row(ln_ffn_b[l])) for l in range(DEPTH)],
    )
    Bp, Tp, D = x_prompt.shape
    Ts = x_sample.shape[1]
    past_len = cache_k.shape[1]
    shift0 = jnp.zeros((1, Bp, 1, D), x_prompt.dtype)
    wkv0 = jnp.zeros((1, Bp, RWKV_HEADS, RWKV_HEAD, RWKV_HEAD), x_prompt.dtype)
    y_p, sh_p, wkv_p, ck_p, cv_p = _trunk(x_prompt, jnp.arange(Tp, dtype=jnp.int32), shift0, wkv0, None, None, P)
    y_s, sh_s, wkv_s, ck_s, cv_s = _trunk(x_sample, past_len + jnp.arange(Ts, dtype=jnp.int32),
                                          state_shift, state_wkv, cache_k, cache_v, P)
    return (y_p, y_s, wkv_p, sh_p, ck_p, cv_p, wkv_s, sh_s, ck_s, cv_s)
```

```python
import functools
import math

import jax
import jax.numpy as jnp
from jax import lax
from jax.experimental import pallas as pl
from jax.experimental.pallas import tpu as pltpu

F32 = jnp.float32
BF16 = jnp.bfloat16
PK16 = jnp.bfloat16
PK_ROWS = 16

D_MODEL = 1024
DEPTH = 2
CHUNK = 64
RWKV_HEAD = 64
RWKV_HEADS = D_MODEL // RWKV_HEAD
GN_EPS = 64e-5
DIFF_HEADS = 8
DIFF_DH = D_MODEL // (2 * DIFF_HEADS)
ROT_DIM = DIFF_DH // 4
ROPE_THETA = 500000.0
SUBLN_EPS = 1e-5
PEER_HEADS = 8
PEER_KEYS = 128
PEER_DHALF = 128
PEER_TOPK = 16
DN_ALPHA = (2.0 * DEPTH) ** 0.25
LN_EPS = 1e-5
DECAY_SCALE = math.exp(-0.5)
SQRT_HALF = math.sqrt(0.5)

LANES = 128
MIB = 1 << 20


def _cparams(semantics, vmem_mib):
    return pltpu.CompilerParams(dimension_semantics=semantics, vmem_limit_bytes=vmem_mib * MIB)


def _dot(a, b):
    return jnp.dot(a, b, preferred_element_type=F32)


def _dot_nt(a, b):
    return lax.dot_general(a, b, (((1,), (1,)), ((), ())), preferred_element_type=F32)


def _split_bf16(x):
    hi = x.astype(BF16)
    lo = (x - hi.astype(F32)).astype(BF16)
    return hi, lo


def _seg_sum(x, seg_ref, segt_ref):
    hi, lo = _split_bf16(x)
    s = _dot(hi, seg_ref[...]) + _dot(lo, seg_ref[...])
    shi, slo = _split_bf16(s)
    return _dot(shi, segt_ref[...]) + _dot(slo, segt_ref[...])


def _layer_norm(x, g, b):
    mu = jnp.mean(x, axis=-1, keepdims=True)
    xc = x - mu
    var = jnp.mean(xc * xc, axis=-1, keepdims=True)
    return xc * lax.rsqrt(var + LN_EPS) * g + b


def _full(shape):
    nd = len(shape)
    return pl.BlockSpec(shape, lambda *_: (0,) * nd)


def _rwkv_pre_kernel(x_ref, xp_ref, mu_ref, wr_ref, wk_ref, wv_ref, w1_ref, w2_ref, w0_ref,
                     a1_ref, a2_ref, a0_ref, g1_ref, g2_ref, kk_ref, ka_ref, seg_ref, segt_ref,
                     r_o, w_o, k_o, v_o, kk_o, b_o, g_o):
    x = x_ref[...]
    xx = xp_ref[...] - x

    def mix(i):
        return (x + xx * mu_ref[i:i + 1, :]).astype(BF16)

    r = _dot(mix(0), wr_ref[...])
    k = _dot(mix(1), wk_ref[...])
    v = _dot(mix(2), wv_ref[...])
    z = w0_ref[...] + _dot(jnp.tanh(_dot(mix(3), w1_ref[...])).astype(BF16), w2_ref[...])
    w = jnp.exp(-DECAY_SCALE * jax.nn.sigmoid(z))
    a = jax.nn.sigmoid(a0_ref[...] + _dot(_dot(mix(4), a1_ref[...]).astype(BF16), a2_ref[...]))
    g = _dot(jax.nn.sigmoid(_dot(mix(5), g1_ref[...])).astype(BF16), g2_ref[...])
    kk = k * kk_ref[...]
    nrm = jnp.sqrt(_seg_sum(kk * kk, seg_ref, segt_ref))
    kk = kk / jnp.maximum(nrm, 1e-12)
    r_o[...] = r
    w_o[...] = w
    k_o[...] = k * (1.0 + (a - 1.0) * ka_ref[...])
    v_o[...] = v
    kk_o[...] = kk
    b_o[...] = kk * a
    g_o[...] = g


def _rwkv_pre(x, xp, P, tn):
    n = x.shape[0]
    row = pl.BlockSpec((tn, D_MODEL), lambda i: (i, 0))
    consts = [P['mu'], P['wr'], P['wk'], P['wv'], P['w1'], P['w2'], P['w0'], P['a1'], P['a2'], P['a0'],
              P['g1'], P['g2'], P['k_k'], P['k_a'], P['seg'], P['segt']]
    out = jax.ShapeDtypeStruct((n, D_MODEL), F32)
    return pl.pallas_call(
        _rwkv_pre_kernel,
        grid=(n // tn,),
        in_specs=[row, row] + [_full(c.shape) for c in consts],
        out_specs=[row] * 7,
        out_shape=[out] * 7,
        compiler_params=_cparams(("parallel",), 48),
        name="rwkv_pre",
    )(x, xp, *consts)


def _rwkv_scan_kernel(r_ref, w_ref, k_ref, v_ref, kk_ref, b_ref, s0_ref, y_ref, s_ref, *, tc):
    @pl.when(pl.program_id(1) == 0)
    def _():
        s_ref[...] = s0_ref[...]

    def step(t, carry):
        kk_t = kk_ref[t]
        b_t = b_ref[t]
        k_t = k_ref[t]
        w_t = w_ref[t]
        r_t = r_ref[t]

        def vrow(vi, c):
            s = s_ref[vi]
            skk = jnp.sum(s * kk_t, axis=0, keepdims=True)
            vv = v_ref[t, pl.ds(vi, 1), :]
            s = s * w_t - skk * b_t + vv * k_t
            s_ref[vi] = s
            y_ref[t, pl.ds(vi, 1), :] = jnp.sum(s * r_t, axis=0, keepdims=True)
            return c

        return lax.fori_loop(0, RWKV_HEAD, vrow, carry, unroll=4)

    lax.fori_loop(0, tc, step, 0)


def _rwkv_scan(r, w, k, v, kk, b, s0, tc):
    T, N, L = r.shape
    seq = pl.BlockSpec((tc, N, LANES), lambda g, i: (i, 0, g))
    st = pl.BlockSpec((N, N, LANES), lambda g, i: (0, 0, g))
    return pl.pallas_call(
        functools.partial(_rwkv_scan_kernel, tc=tc),
        grid=(L // LANES, T // tc),
        in_specs=[seq] * 6 + [st],
        out_specs=[seq, st],
        out_shape=[jax.ShapeDtypeStruct((T, N, L), F32), jax.ShapeDtypeStruct((N, N, L), F32)],
        compiler_params=_cparams(("parallel", "arbitrary"), 48),
        name="rwkv_scan",
    )(r, w, k, v, kk, b, s0)


def _rwkv_post_kernel(y_ref, r_ref, k_ref, v_ref, g_ref, x_ref, rk_ref, lg_ref, lb_ref, wo_ref,
                      ng_ref, nb_ref, seg_ref, segt_ref, x_o, xt_o):
    y = y_ref[...]
    inv_n = 1.0 / RWKV_HEAD
    yc = y - _seg_sum(y, seg_ref, segt_ref) * inv_n
    var = _seg_sum(yc * yc, seg_ref, segt_ref) * inv_n
    yn = yc * lax.rsqrt(var + GN_EPS) * lg_ref[...] + lb_ref[...]
    bonus = _seg_sum(r_ref[...] * k_ref[...] * rk_ref[...], seg_ref, segt_ref) * v_ref[...]
    o = _dot(((yn + bonus) * g_ref[...]).astype(BF16), wo_ref[...])
    xn = _layer_norm(DN_ALPHA * x_ref[...] + o, ng_ref[...], nb_ref[...])
    x_o[...] = xn
    xt_o[...] = xn.T.astype(BF16)


def _rwkv_post(y, r, k, v, g, x, P, tn):
    n = x.shape[0]
    row = pl.BlockSpec((tn, D_MODEL), lambda i: (i, 0))
    consts = [P['r_k'], P['lnx_g'], P['lnx_b'], P['w_out'], P['ln_mix_g0'], P['ln_mix_b0'], P['seg'], P['segt']]
    return pl.pallas_call(
        _rwkv_post_kernel,
        grid=(n // tn,),
        in_specs=[row] * 6 + [_full(c.shape) for c in consts],
        out_specs=[row, pl.BlockSpec((D_MODEL, tn), lambda i: (0, i))],
        out_shape=[jax.ShapeDtypeStruct((n, D_MODEL), F32), jax.ShapeDtypeStruct((D_MODEL, n), BF16)],
        compiler_params=_cparams(("parallel",), 48),
        name="rwkv_post",
    )(y, r, k, v, g, x, *consts)


def _top_values(s, count):
    vals = []
    cur = s
    for _ in range(count):
        m = jnp.max(cur, axis=0, keepdims=True)
        vals.append(m)
        cur = jnp.where(cur == m, -jnp.inf, cur)
    return vals


def _pack_pair(x):
    bits = pltpu.bitcast(x.astype(PK16).astype(F32), jnp.uint32)
    return bits | (bits >> 16)


def _pack_rows(even, odd):
    lo = pltpu.bitcast(even.astype(PK16).astype(F32), jnp.uint32) >> 16
    hi = pltpu.bitcast(odd.astype(PK16).astype(F32), jnp.uint32) & jnp.uint32(0xFFFF0000)
    return lo | hi


def _peer_select_kernel(xt_ref, wq_ref, sk_ref, e1_o, c_o, e2_o, r2_o):
    qt = _dot(wq_ref[...], xt_ref[...])
    tn = qt.shape[1]
    row8 = lax.broadcasted_iota(jnp.int32, (8, tn), 0)
    nv = PEER_TOPK + 1
    for h in range(PEER_HEADS):
        q1 = qt[(2 * h) * PEER_DHALF:(2 * h + 1) * PEER_DHALF].astype(BF16)
        q2 = qt[(2 * h + 1) * PEER_DHALF:(2 * h + 2) * PEER_DHALF].astype(BF16)
        s1 = _dot(sk_ref[0], q1)
        s2 = _dot(sk_ref[1], q2)
        v1 = _top_values(s1, nv)
        v2 = _top_values(s2, nv)
        v1a = jnp.concatenate(v1[:PEER_TOPK], axis=0)
        cands = [v1a[0:8] + v2[0], v1a[8:16] + v2[0]]
        for bi in range(1, PEER_TOPK):
            cnt = nv // (bi + 1)
            cands.append(jnp.where(row8 < cnt, v1a[0:8] + v2[bi], -jnp.inf))
        cands.append(jnp.where(row8 == 0, v1[0] + v2[PEER_TOPK],
                               jnp.where(row8 == 1, v1[PEER_TOPK] + v2[0], -jnp.inf)))
        tops = []
        for _ in range(nv):
            m = cands[0]
            for c in cands[1:]:
                m = jnp.maximum(m, c)
            m = jnp.max(m, axis=0, keepdims=True)
            tops.append(m)
            cands = [jnp.where(c == m, -jnp.inf, c) for c in cands]
        z = jnp.zeros_like(tops[0])
        for t in tops[:PEER_TOPK]:
            z = z + jnp.exp(t - tops[0])
        inv_z = 1.0 / z
        tau = 0.5 * (tops[PEER_TOPK - 1] + tops[PEER_TOPK])
        cnt = jnp.zeros_like(s1)
        rank = jnp.zeros_like(s2)
        for bi in range(PEER_TOPK):
            cnt = cnt + jnp.where(s1 + v2[bi] >= tau, 1.0, 0.0)
            rank = rank + jnp.where(v2[bi] > s2, 1.0, 0.0)
        e2 = jnp.exp(s2 - v2[0]) * inv_z
        half = PK_ROWS // 2
        e2_words, r2_words = [], []
        for jg in range(PEER_KEYS // PK_ROWS):
            lo = slice(jg * PK_ROWS, jg * PK_ROWS + half)
            hi = slice(jg * PK_ROWS + half, (jg + 1) * PK_ROWS)
            e2_words.append(_pack_rows(e2[lo], e2[hi]))
            r2_words.append(_pack_rows(rank[lo], rank[hi]))
        outs = ((e1_o, _pack_pair(jnp.exp(s1 - v1[0]))), (c_o, _pack_pair(cnt)),
                (e2_o, jnp.concatenate(e2_words, axis=0)), (r2_o, jnp.concatenate(r2_words, axis=0)))
        for ref, val in outs:
            for lc in range(tn // LANES):
                ref[lc, h] = val[:, lc * LANES:(lc + 1) * LANES]


def _peer_select(xt, wqt, sk, tn):
    n = xt.shape[1]
    nc = tn // LANES
    per_first = pl.BlockSpec((nc, PEER_HEADS, PEER_KEYS, LANES), lambda i: (i, 0, 0, 0))
    per_second = pl.BlockSpec((nc, PEER_HEADS, PEER_KEYS // 2, LANES), lambda i: (i, 0, 0, 0))
    first = jax.ShapeDtypeStruct((n // LANES, PEER_HEADS, PEER_KEYS, LANES), jnp.uint32)
    second = jax.ShapeDtypeStruct((n // LANES, PEER_HEADS, PEER_KEYS // 2, LANES), jnp.uint32)
    return pl.pallas_call(
        _peer_select_kernel,
        grid=(n // tn,),
        in_specs=[pl.BlockSpec((D_MODEL, tn), lambda i: (0, i)), _full(wqt.shape), _full(sk.shape)],
        out_specs=[per_first, per_first, per_second, per_second],
        out_shape=[first, first, second, second],
        compiler_params=_cparams(("parallel",), 48),
        name="peer_select",
    )(xt, wqt, sk)


def _second_key_order():
    g = jnp.arange(PEER_KEYS) // PK_ROWS
    r = jnp.arange(PEER_KEYS) % PK_ROWS
    half = PK_ROWS // 2
    return g * PK_ROWS + jnp.where(r < half, 2 * r, 2 * (r - half) + 1)


PEER_ROWS_PER_BLOCK = 8


def _peer_dense_kernel(xt_ref, u_ref, vt_ref, e1_ref, c_ref, e2_ref, r2_ref, x_ref, ng_ref, nb_ref, x_o,
                       g_a, g_b, h_a, h_b, acc_scr, *, tn, n_eb):
    s = pl.program_id(0)

    @pl.when(s == 0)
    def _():
        g_a[...] = jnp.zeros_like(g_a)
        g_b[...] = jnp.zeros_like(g_b)
        h_a[...] = jnp.zeros_like(h_a)
        h_b[...] = jnp.zeros_like(h_b)
        acc_scr[...] = jnp.zeros_like(acc_scr)

    nc = tn // LANES

    def lane_chunks(val):
        return [val[:, lc * LANES:(lc + 1) * LANES] for lc in range(nc)]

    def stages(g_w, g_r, h_w, h_r):
        for lc, part in enumerate(lane_chunks(_dot(u_ref[...], xt_ref[...]))):
            g_w[lc] = part
        hidden = jnp.concatenate([h_r[lc] for lc in range(nc)], axis=1)
        for lc, part in enumerate(lane_chunks(_dot(vt_ref[...], hidden))):
            acc_scr[lc] += part
        zero = jnp.zeros((PK_ROWS, LANES), PK16)
        for lc in range(nc):
            for ii in range(PEER_ROWS_PER_BLOCK):
                def row_pair(ref, h):
                    words = jnp.broadcast_to(ref[lc, h, ii:ii + 1, :], (PK_ROWS // 2, LANES))
                    return pltpu.bitcast(words, PK16)
                cnt = [row_pair(c_ref, h) for h in range(PEER_HEADS)]
                e1 = [row_pair(e1_ref, h) for h in range(PEER_HEADS)]
                for jg in range(PEER_KEYS // PK_ROWS):
                    jr = slice(jg * (PK_ROWS // 2), (jg + 1) * (PK_ROWS // 2))
                    coef = zero
                    for h in range(PEER_HEADS):
                        r2 = pltpu.bitcast(r2_ref[lc, h, jr, :], PK16)
                        e2 = pltpu.bitcast(e2_ref[lc, h, jr, :], PK16)
                        coef = coef + jnp.where(r2 < cnt[h], e2, zero) * e1[h]
                    rows = slice(ii * PEER_KEYS + jg * PK_ROWS, ii * PEER_KEYS + (jg + 1) * PK_ROWS)
                    gv = g_r[lc, rows, :]
                    act = 0.5 * gv * (1.0 + lax.erf(gv * SQRT_HALF))
                    h_w[lc, rows, :] = (act.astype(PK16) * coef).astype(BF16)

    @pl.when(s % 2 == 0)
    def _():
        stages(g_a, g_b, h_b, h_a)

    @pl.when(s % 2 == 1)
    def _():
        stages(g_b, g_a, h_a, h_b)

    @pl.when(jnp.logical_and(s >= 2, (s - 2) % n_eb == n_eb - 1))
    def _():
        ffn = jnp.concatenate([acc_scr[lc] for lc in range(nc)], axis=1).T
        x_o[...] = _layer_norm(DN_ALPHA * x_ref[...] + ffn, ng_ref[...], nb_ref[...])
        acc_scr[...] = jnp.zeros_like(acc_scr)


def _peer_dense(xt, u, vt, e1, cnt, e2, r2, x, ng, nb, tn):
    n = x.shape[0]
    eb = PEER_ROWS_PER_BLOCK * PEER_KEYS
    n_eb = u.shape[0] // eb
    n_t = n // tn
    total = n_t * n_eb

    def tile_block(lag):
        def f(s):
            sl = jnp.clip(s - lag, 0, total - 1)
            return sl // n_eb, sl % n_eb
        return f

    st_a, st_b, st_c = tile_block(0), tile_block(1), tile_block(2)
    nc = tn // LANES
    rows1 = pl.BlockSpec((nc, PEER_HEADS, PEER_ROWS_PER_BLOCK, LANES), lambda s: (st_b(s)[0], 0, st_b(s)[1], 0))
    rows2 = pl.BlockSpec((nc, PEER_HEADS, PEER_KEYS // 2, LANES), lambda s: (st_b(s)[0], 0, 0, 0))
    return pl.pallas_call(
        functools.partial(_peer_dense_kernel, tn=tn, n_eb=n_eb),
        grid=(total + 2,),
        in_specs=[pl.BlockSpec((D_MODEL, tn), lambda s: (0, st_a(s)[0])),
                  pl.BlockSpec((eb, D_MODEL), lambda s: (st_a(s)[1], 0)),
                  pl.BlockSpec((D_MODEL, eb), lambda s: (0, st_c(s)[1])),
                  rows1, rows1, rows2, rows2,
                  pl.BlockSpec((tn, D_MODEL), lambda s: (st_c(s)[0], 0)),
                  _full(ng.shape), _full(nb.shape)],
        out_specs=pl.BlockSpec((tn, D_MODEL), lambda s: (st_c(s)[0], 0)),
        out_shape=jax.ShapeDtypeStruct((n, D_MODEL), F32),
        scratch_shapes=[pltpu.VMEM((nc, eb, LANES), F32), pltpu.VMEM((nc, eb, LANES), F32),
                        pltpu.VMEM((nc, eb, LANES), BF16), pltpu.VMEM((nc, eb, LANES), BF16),
                        pltpu.VMEM((nc, D_MODEL, LANES), F32)],
        compiler_params=_cparams(("arbitrary",), 56),
        name="peer_dense",
    )(xt, u, vt, e1, cnt, e2, r2, x, ng, nb)


def _peer_ffn(x, xt, L, tn_sel, tn_dense):
    e1, cnt, e2, r2 = _peer_select(xt, L['wqt'], L['sk'], tn_sel)
    return _peer_dense(xt, L['u'], L['vt'], e1, cnt, e2, r2, x, L['ln_ffn_g'], L['ln_ffn_b'], tn_dense)


def _rope(t, c_ref, sa_ref, sb_ref):
    half = ROT_DIM // 2
    up = pltpu.roll(t, D_MODEL - half, 1)
    down = pltpu.roll(t, half, 1)
    return t * c_ref[...] + up * sa_ref[...] + down * sb_ref[...]


def _kvq_kernel(x_ref, wkv_ref, wq_ref, c_ref, sa_ref, sb_ref, k_o, v_o, qb_o, kb_o, vb_o):
    x = x_ref[0].astype(BF16)
    kv = _dot(x, wkv_ref[...])
    k = _rope(kv[:, :D_MODEL], c_ref, sa_ref, sb_ref)
    v = kv[:, D_MODEL:]
    q = _rope(_dot(x, wq_ref[...]), c_ref, sa_ref, sb_ref)
    k_o[0] = k
    v_o[0] = v
    qb_o[0] = (q * (DIFF_DH ** -0.5)).astype(BF16)
    kb_o[0] = k.astype(BF16)
    vb_o[0] = v.astype(BF16)


def _kvq_proj(x, wkv, wq, rc, rsa, rsb, tt):
    B, T, _ = x.shape
    blk = pl.BlockSpec((1, tt, D_MODEL), lambda b, j: (b, j, 0))
    tab = pl.BlockSpec((tt, D_MODEL), lambda b, j: (j, 0))
    f = jax.ShapeDtypeStruct((B, T, D_MODEL), F32)
    h = jax.ShapeDtypeStruct((B, T, D_MODEL), BF16)
    return pl.pallas_call(
        _kvq_kernel,
        grid=(B, T // tt),
        in_specs=[blk, _full(wkv.shape), _full(wq.shape), tab, tab, tab],
        out_specs=[blk] * 5,
        out_shape=[f, f, h, h, h],
        compiler_params=_cparams(("parallel", "parallel"), 48),
        name="kvq_proj",
    )(x, wkv, wq, rc, rsa, rsb)


def _bcast_cols(m, width):
    if width <= LANES:
        return m[:, :width]
    return jnp.concatenate([m] * (width // LANES), axis=1)


def _split_maps(q):
    lane = lax.broadcasted_iota(jnp.int32, q.shape, 1)
    zero = jnp.zeros_like(q)
    return jnp.where(lane < DIFF_DH, q, zero), jnp.where(lane >= DIFF_DH, q, zero)


def _attn_update(qs, k, v, m_ref, l_ref, acc_ref, idx, mask):
    for c in range(2):
        s = _dot_nt(qs[c], k)
        if mask is not None:
            s = jnp.where(mask, s, -jnp.inf)
        m_prev = m_ref[idx + c]
        m_new = jnp.maximum(m_prev, jnp.max(s, axis=1, keepdims=True))
        p = jnp.exp(s - _bcast_cols(m_new, s.shape[1]))
        alpha = jnp.exp(m_prev - m_new)
        l_ref[idx + c] = alpha * l_ref[idx + c] + jnp.sum(p, axis=1, keepdims=True)
        acc_ref[idx + c] = alpha * acc_ref[idx + c] + _dot(p.astype(BF16), v)
        m_ref[idx + c] = m_new


def _attn_init(m_ref, l_ref, acc_ref):
    m_ref[...] = jnp.full_like(m_ref, -jnp.inf)
    l_ref[...] = jnp.zeros_like(l_ref)
    acc_ref[...] = jnp.zeros_like(acc_ref)


def _attn_finish(lam, sg, m_ref, l_ref, acc_ref, idx, lam_init):
    o = acc_ref[idx] / l_ref[idx] - lam * (acc_ref[idx + 1] / l_ref[idx + 1])
    o = o * lax.rsqrt(jnp.mean(o * o, axis=-1, keepdims=True) + SUBLN_EPS) * sg
    return (o * (1.0 - lam_init)).astype(BF16)


def _attn_prompt_kernel(lam_ref, q_ref, k_ref, v_ref, sg_ref, o_ref, m_ref, l_ref, acc_ref, *, tq, lam_init):
    qi = pl.program_id(2)
    _attn_init(m_ref, l_ref, acc_ref)
    q = _split_maps(q_ref[0])

    def kv_block(j):
        rows = pl.ds(pl.multiple_of(j * tq, tq), tq)
        return k_ref[0, rows, :], v_ref[0, rows, :]

    def below_diagonal(j, carry):
        k, v = kv_block(j)
        _attn_update(q, k, v, m_ref, l_ref, acc_ref, 0, None)
        return carry

    lax.fori_loop(0, qi, below_diagonal, 0)
    qc = lax.broadcasted_iota(jnp.int32, (tq, tq), 0) // CHUNK
    kc = lax.broadcasted_iota(jnp.int32, (tq, tq), 1) // CHUNK
    k, v = kv_block(qi)
    _attn_update(q, k, v, m_ref, l_ref, acc_ref, 0, kc <= qc)
    o_ref[0] = _attn_finish(lam_ref[0], sg_ref[...], m_ref, l_ref, acc_ref, 0, lam_init)


def _attn_prompt(lam, q, k, v, sg, tq, lam_init):
    B, T, _ = q.shape
    qspec = pl.BlockSpec((1, tq, LANES), lambda b, h, i: (b, i, h))
    kspec = pl.BlockSpec((1, T, LANES), lambda b, h, i: (b, 0, h))
    return pl.pallas_call(
        functools.partial(_attn_prompt_kernel, tq=tq, lam_init=lam_init),
        grid=(B, DIFF_HEADS, T // tq),
        in_specs=[pl.BlockSpec(memory_space=pltpu.SMEM), qspec, kspec, kspec, _full(sg.shape)],
        out_specs=qspec,
        out_shape=jax.ShapeDtypeStruct((B, T, D_MODEL), BF16),
        scratch_shapes=[pltpu.VMEM((2, tq, LANES), F32)] * 3,
        compiler_params=_cparams(("parallel", "parallel", "arbitrary"), 32),
        name="attn_prompt",
    )(lam, q, k, v, sg)


def _attn_sample_kernel(lam_ref, q_ref, ck_ref, cv_ref, nk_ref, nv_ref, sg_ref, o_ref,
                        m_ref, l_ref, acc_ref, *, lam_init):
    j = pl.program_id(1)

    @pl.when(j == 0)
    def _():
        _attn_init(m_ref, l_ref, acc_ref)

    for h in range(DIFF_HEADS):
        cols = slice(h * LANES, (h + 1) * LANES)
        _attn_update(_split_maps(q_ref[0, :, cols]), ck_ref[0, :, cols].astype(BF16),
                     cv_ref[0, :, cols].astype(BF16), m_ref, l_ref, acc_ref, 2 * h, None)

    @pl.when(j == pl.num_programs(1) - 1)
    def _():
        for h in range(DIFF_HEADS):
            cols = slice(h * LANES, (h + 1) * LANES)
            _attn_update(_split_maps(q_ref[0, :, cols]), nk_ref[0, :, cols], nv_ref[0, :, cols],
                         m_ref, l_ref, acc_ref, 2 * h, None)
            o_ref[0, :, cols] = _attn_finish(lam_ref[0], sg_ref[...], m_ref, l_ref, acc_ref, 2 * h, lam_init)


def _attn_sample(lam, q, ck, cv, nk, nv, sg, tk, lam_init):
    B, T, _ = q.shape
    past = ck.shape[1]
    new = pl.BlockSpec((1, T, D_MODEL), lambda b, j: (b, 0, 0))
    old = pl.BlockSpec((1, tk, D_MODEL), lambda b, j: (b, j, 0))
    return pl.pallas_call(
        functools.partial(_attn_sample_kernel, lam_init=lam_init),
        grid=(B, past // tk),
        in_specs=[pl.BlockSpec(memory_space=pltpu.SMEM), new, old, old, new, new, _full(sg.shape)],
        out_specs=new,
        out_shape=jax.ShapeDtypeStruct((B, T, D_MODEL), BF16),
        scratch_shapes=[pltpu.VMEM((2 * DIFF_HEADS, T, LANES), F32)] * 3,
        compiler_params=_cparams(("parallel", "arbitrary"), 48),
        name="attn_sample",
    )(lam, q, ck, cv, nk, nv, sg)


def _attn_out_kernel(o_ref, x_ref, wo_ref, ng_ref, nb_ref, x_o, xt_o):
    mixv = _dot(o_ref[...], wo_ref[...])
    xn = _layer_norm(DN_ALPHA * x_ref[...] + mixv, ng_ref[...], nb_ref[...])
    x_o[...] = xn
    xt_o[...] = xn.T.astype(BF16)


def _attn_out(o, x, wo, ng, nb, tn):
    n = x.shape[0]
    row = pl.BlockSpec((tn, D_MODEL), lambda i: (i, 0))
    return pl.pallas_call(
        _attn_out_kernel,
        grid=(n // tn,),
        in_specs=[row, row, _full(wo.shape), _full(ng.shape), _full(nb.shape)],
        out_specs=[row, pl.BlockSpec((D_MODEL, tn), lambda i: (0, i))],
        out_shape=[jax.ShapeDtypeStruct((n, D_MODEL), F32), jax.ShapeDtypeStruct((D_MODEL, n), BF16)],
        compiler_params=_cparams(("parallel",), 32),
        name="attn_out",
    )(o, x, wo, ng, nb)


def _rope_tables(pos):
    half = ROT_DIM // 2
    inv = jnp.power(ROPE_THETA, -jnp.arange(half, dtype=F32) * 2.0 / ROT_DIM)
    ang = pos.astype(F32)[:, None] * inv[None, :]
    cos, sin = jnp.cos(ang), jnp.sin(ang)
    T = pos.shape[0]
    rest = DIFF_DH - ROT_DIM
    c = jnp.concatenate([cos, cos, jnp.ones((T, rest), F32)], axis=1)
    sa = jnp.concatenate([-sin, jnp.zeros((T, half + rest), F32)], axis=1)
    sb = jnp.concatenate([jnp.zeros((T, half), F32), sin, jnp.zeros((T, rest), F32)], axis=1)
    reps = D_MODEL // DIFF_DH
    return jnp.tile(c, (1, reps)), jnp.tile(sa, (1, reps)), jnp.tile(sb, (1, reps))


def _pick(n, prefs):
    for t in prefs:
        if n % t == 0:
            return t
    return n


def _trunk(x, pos, shift_prev, wkv_prev, past_k, past_v, P):
    B, T, _ = x.shape
    n = B * T
    H, N = RWKV_HEADS, RWKV_HEAD
    x2d = x.reshape(n, D_MODEL)
    tn = _pick(n, (256,))

    xprev = jnp.concatenate([shift_prev[0].astype(x.dtype), x[:, :-1]], axis=1).reshape(n, D_MODEL)
    r, w, k, v, kk, b, g = _rwkv_pre(x2d, xprev, P, tn)

    def to_scan(a):
        return a.reshape(B, T, H, N).transpose(1, 3, 0, 2).reshape(T, N, B * H)

    s0 = wkv_prev[0].astype(F32).transpose(2, 3, 0, 1).reshape(N, N, B * H)
    y, s_fin = _rwkv_scan(to_scan(r), to_scan(w), to_scan(k), to_scan(v), to_scan(kk), to_scan(b), s0,
                          _pick(T, (16,)))
    y = y.reshape(T, N, B, H).transpose(2, 0, 3, 1).reshape(n, D_MODEL)
    new_wkv = s_fin.reshape(N, N, B, H).transpose(2, 3, 0, 1)[None].astype(x.dtype)
    new_shift = x[:, -1:][None]
    x1, x1t = _rwkv_post(y, r, k, v, g, x2d, P, tn)
    tn_dense = _pick(n, (512, 256))
    x2 = _peer_ffn(x1, x1t, P['peer'][0], tn, tn_dense)

    rc, rsa, rsb = _rope_tables(pos)
    k_new, v_new, qb, kb, vb = _kvq_proj(x2.reshape(B, T, D_MODEL), P['w_kv'], P['w_q'], rc, rsa, rsb,
                                         _pick(T, (256,)))
    lam_init = 0.8 - 0.6 * math.exp(-0.3 * 1)
    if past_k is None:
        o = _attn_prompt(P['lam'], qb, kb, vb, P['subln_g'], _pick(T, (512, 256)), lam_init)
    else:
        past = past_k.shape[1]
        o = _attn_sample(P['lam'], qb, past_k.reshape(B, past, D_MODEL), past_v.reshape(B, past, D_MODEL),
                         kb, vb, P['subln_g'], _pick(past, (512,)), lam_init)
    x3, x3t = _attn_out(o.reshape(n, D_MODEL), x2, P['w_o'], P['ln_mix_g1'], P['ln_mix_b1'], tn)
    y_out = _peer_ffn(x3, x3t, P['peer'][1], tn, tn_dense)
    return (y_out.reshape(B, T, D_MODEL), new_shift, new_wkv,
            k_new.reshape(B, T, DIFF_HEADS, 2, DIFF_DH), v_new.reshape(B, T, DIFF_HEADS, 2 * DIFF_DH))


def kernel(x_prompt, x_sample, state_wkv, state_shift, cache_k, cache_v, rwkv_mu, rwkv_w_rkv, rwkv_w0, rwkv_w1, rwkv_w2, rwkv_a0, rwkv_a1, rwkv_a2, rwkv_g1, rwkv_g2, rwkv_k_k, rwkv_k_a, rwkv_r_k, rwkv_lnx_g, rwkv_lnx_b, rwkv_w_out, w_kv, diff_w_q, diff_lambda, diff_subln_g, diff_w_o, ln_mix_g, ln_mix_b, ln_ffn_g, ln_ffn_b, peer_w_query, peer_sub_keys, peer_u, peer_v):
    row = lambda a: a.reshape(1, -1).astype(F32)
    head_of_lane = jnp.arange(D_MODEL) // RWKV_HEAD
    seg = (head_of_lane[:, None] == jnp.arange(LANES)[None, :]).astype(BF16)
    lp = diff_lambda[0].astype(F32)
    lam_init = 0.8 - 0.6 * math.exp(-0.3 * 1)
    lam = jnp.exp(jnp.sum(lp[0] * lp[1])) - jnp.exp(jnp.sum(lp[2] * lp[3])) + lam_init
    P = dict(
        mu=rwkv_mu[0], wr=rwkv_w_rkv[0, 0].astype(BF16), wk=rwkv_w_rkv[0, 1].astype(BF16),
        wv=rwkv_w_rkv[0, 2].astype(BF16), w1=rwkv_w1[0].astype(BF16), w2=rwkv_w2[0].astype(BF16),
        w0=row(rwkv_w0[0]), a1=rwkv_a1[0].astype(BF16), a2=rwkv_a2[0].astype(BF16), a0=row(rwkv_a0[0]),
        g1=rwkv_g1[0].astype(BF16), g2=rwkv_g2[0].astype(BF16), k_k=row(rwkv_k_k[0]), k_a=row(rwkv_k_a[0]),
        r_k=row(rwkv_r_k[0]), lnx_g=row(rwkv_lnx_g[0]), lnx_b=row(rwkv_lnx_b[0]),
        w_out=rwkv_w_out[0].astype(BF16), seg=seg, segt=seg.T,
        ln_mix_g0=row(ln_mix_g[0]), ln_mix_b0=row(ln_mix_b[0]),
        ln_mix_g1=row(ln_mix_g[1]), ln_mix_b1=row(ln_mix_b[1]),
        w_kv=w_kv.astype(BF16), w_q=diff_w_q[0].astype(BF16), w_o=diff_w_o[0].astype(BF16),
        subln_g=row(diff_subln_g[0]), lam=lam.reshape(1).astype(F32),
        peer=[dict(wqt=peer_w_query[l].T.astype(BF16),
                   sk=jnp.stack([peer_sub_keys[l, 0], peer_sub_keys[l, 1][_second_key_order()]]).astype(BF16),
                   u=peer_u[l].astype(BF16), vt=peer_v[l].T.astype(BF16),
                   ln_ffn_g=row(ln_ffn_g[l]), ln_ffn_b=row(ln_ffn_b[l])) for l in range(DEPTH)],
    )
    Bp, Tp, D = x_prompt.shape
    Ts = x_sample.shape[1]
    past_len = cache_k.shape[1]
    shift0 = jnp.zeros((1, Bp, 1, D), x_prompt.dtype)
    wkv0 = jnp.zeros((1, Bp, RWKV_HEADS, RWKV_HEAD, RWKV_HEAD), x_prompt.dtype)
    y_p, sh_p, wkv_p, ck_p, cv_p = _trunk(x_prompt, jnp.arange(Tp, dtype=jnp.int32), shift0, wkv0, None, None, P)
    y_s, sh_s, wkv_s, ck_s, cv_s = _trunk(x_sample, past_len + jnp.arange(Ts, dtype=jnp.int32),
                                          state_shift, state_wkv, cache_k, cache_v, P)
    return (y_p, y_s, wkv_p, sh_p, ck_p, cv_p, wkv_s, sh_s, ck_s, cv_s)
```
